```python
import math
import jax, jax.numpy as jnp
from jax import lax
import numpy as np

D_MODEL = 4096
BATCH = 2
SEQ = 4096
DEPTH = 2
DEC_BATCH = 2
DEC_SEQ = 8192
PAST_LEN = 128

A_WIDTH = D_MODEL // 2
A_CONV = 3
B_WIDTH = D_MODEL // 2
B_CONV = 31
C_HEADS = 8
C_HEAD_DIM = 128
C_WIDTH = C_HEADS * 2 * C_HEAD_DIM
ROPE_THETA = 10000.0
Q_BLOCK = 128
N_BRANCH = 3
COL_A = 3 * A_WIDTH
COL_B = 2 * B_WIDTH
COL_C = 3 * C_WIDTH
COL_G = N_BRANCH * D_MODEL
IN_COLS = COL_A + COL_B + COL_C + COL_G
PEER_HEADS = 8
PEER_NKEYS = 128
PEER_EXPERTS = PEER_NKEYS * PEER_NKEYS
PEER_QDIM = 256
PEER_HALF = PEER_QDIM // 2
PEER_TOPK = 16
PEER_CHUNK = 128
EPS = 1e-6

kernel_name = 'hybrid_gated_conv_diffattn_peer_encoder'


def rms_norm(x, g):
    xf = x.astype(jnp.float32)
    y = xf * lax.rsqrt(jnp.mean(xf * xf, axis=-1, keepdims=True) + EPS)
    return (y * g.astype(jnp.float32)).astype(x.dtype)


def layer_norm(x, g, b):
    xf = x.astype(jnp.float32)
    mu = jnp.mean(xf, axis=-1, keepdims=True)
    xc = xf - mu
    y = xc * lax.rsqrt(jnp.mean(xc * xc, axis=-1, keepdims=True) + EPS)
    return (y * g.astype(jnp.float32) + b.astype(jnp.float32)).astype(x.dtype)


def depthwise_conv(x, w):
    k = w.shape[0]
    return lax.conv_general_dilated(
        x, w[:, None, :].astype(x.dtype), window_strides=(1,),
        padding=[(k // 2, k // 2)], dimension_numbers=('NWC', 'WIO', 'NWC'),
        feature_group_count=x.shape[-1])


def rope_tables(seq, dim):
    inv = 1.0 / (ROPE_THETA ** (jnp.arange(0, dim, 2, dtype=jnp.float32) / dim))
    ang = jnp.arange(seq, dtype=jnp.float32)[:, None] * inv[None, :]
    return jnp.cos(ang), jnp.sin(ang)


def apply_rope(x, cos, sin):
    x1, x2 = jnp.split(x, 2, axis=-1)
    c = cos[:, None, None, :]
    s = sin[:, None, None, :]
    return jnp.concatenate([x1 * c - x2 * s, x1 * s + x2 * c], axis=-1).astype(x.dtype)


def diff_attention(q, k, v, lam):
    B, S, H, _, d = q.shape
    scale = d ** -0.5
    nblk = S // Q_BLOCK
    qb = q.reshape(B, nblk, Q_BLOCK, H, 2, d).transpose(1, 0, 2, 3, 4, 5)

    def block(qi):
        s = jnp.einsum('bqhmd,bkhmd->bhmqk', qi, k).astype(jnp.float32) * scale
        p = jax.nn.softmax(s, axis=-1)
        w = p[:, :, 0] - lam * p[:, :, 1]
        return jnp.einsum('bhqk,bkhe->bqhe', w.astype(v.dtype), v)

    o = lax.map(block, qb)
    return o.transpose(1, 0, 2, 3, 4).reshape(B, S, H, 2 * d)


def peer_ffn(h, w_q, keys, u, v):
    T, D = h.shape
    q = jnp.einsum('td,de->te', h, w_q).reshape(T, PEER_HEADS, 2, PEER_HALF)
    s = jnp.einsum('thpd,hpnd->thpn', q, keys).astype(jnp.float32)
    top_s, top_i = lax.top_k(s, PEER_TOPK)
    cand_s = top_s[:, :, 0, :, None] + top_s[:, :, 1, None, :]
    cand_i = top_i[:, :, 0, :, None] * PEER_NKEYS + top_i[:, :, 1, None, :]
    cand_s = cand_s.reshape(T, PEER_HEADS, PEER_TOPK * PEER_TOPK)
    cand_i = cand_i.reshape(T, PEER_HEADS, PEER_TOPK * PEER_TOPK)
    best_s, pos = lax.top_k(cand_s, PEER_TOPK)
    idx = jnp.take_along_axis(cand_i, pos, axis=-1)
    g = jax.nn.softmax(best_s, axis=-1)
    nch = T // PEER_CHUNK

    def chunk(args):
        hc, ic, gc = args
        a = jnp.einsum('cd,chkd->chk', hc, u[ic])
        coef = (gc * jax.nn.gelu(a.astype(jnp.float32), approximate=False)).astype(h.dtype)
        return jnp.einsum('chk,chkd->cd', coef, v[ic])

    out = lax.map(chunk, (h.reshape(nch, PEER_CHUNK, D),
                          idx.reshape(nch, PEER_CHUNK, PEER_HEADS, PEER_TOPK),
                          g.reshape(nch, PEER_CHUNK, PEER_HEADS, PEER_TOPK)))
    return out.reshape(T, D)


def encoder_layer(x, c, l, cos, sin, params):
    (w_ada, b_ada, norm1_g, norm2_g, w_in, conv_a, w_a_out, conv_b, ln_b_g, ln_b_b,
     w_b_out, lambda_qk, subln_g, w_c_out, w_out, peer_wq, peer_keys, peer_u, peer_v) = params
    B, S, D = x.shape
    mod = jnp.einsum('bd,de->be', jax.nn.silu(c), w_ada[l]) + b_ada[l]
    sh1, sc1, gt1, sh2, sc2, gt2 = jnp.split(mod[:, None, :], 6, axis=-1)

    h = rms_norm(x, norm1_g[l]) * (1 + sc1) + sh1
    z = jnp.einsum('bsd,de->bse', h, w_in[l])
    za, zb, zc, zg = jnp.split(z, [COL_A, COL_A + COL_B, COL_A + COL_B + COL_C], axis=-1)
    a_h, a_b, a_c = jnp.split(za, 3, axis=-1)
    y_a = jnp.einsum('bsc,cd->bsd', a_b * depthwise_conv(a_c * a_h, conv_a[l]), w_a_out[l])
    b_val, b_gate = jnp.split(zb, 2, axis=-1)
    b_dw = depthwise_conv(b_val * jax.nn.sigmoid(b_gate), conv_b[l])
    y_b = jnp.einsum('bsc,cd->bsd', jax.nn.silu(layer_norm(b_dw, ln_b_g[l], ln_b_b[l])), w_b_out[l])
    c_q, c_k, c_v = jnp.split(zc, 3, axis=-1)
    q = apply_rope(c_q.reshape(B, S, C_HEADS, 2, C_HEAD_DIM), cos, sin)
    k = apply_rope(c_k.reshape(B, S, C_HEADS, 2, C_HEAD_DIM), cos, sin)
    v = c_v.reshape(B, S, C_HEADS, 2 * C_HEAD_DIM)
    lam_init = 0.8 - 0.6 * math.exp(-0.3 * l)
    lq = lambda_qk[l].astype(jnp.float32)
    lam = jnp.exp(jnp.sum(lq[0] * lq[1])) - jnp.exp(jnp.sum(lq[2] * lq[3])) + lam_init
    o = rms_norm(diff_attention(q, k, v, lam), subln_g[l]) * (1.0 - lam_init)
    y_c = jnp.einsum('bsc,cd->bsd', o.reshape(B, S, C_WIDTH), w_c_out[l])
    gates = jax.nn.sigmoid(zg).reshape(B, S, N_BRANCH, D)
    merged = gates[:, :, 0] * y_a + gates[:, :, 1] * y_b + gates[:, :, 2] * y_c
    x = x + gt1 * jnp.einsum('bsd,de->bse', merged, w_out[l])

    h2 = rms_norm(x, norm2_g[l]) * (1 + sc2) + sh2
    y = peer_ffn(h2.reshape(B * S, D), peer_wq[l], peer_keys[l], peer_u[l], peer_v[l]).reshape(B, S, D)
    return x + gt2 * y


def trunk(x, c, params, final_g):
    cos, sin = rope_tables(x.shape[1], C_HEAD_DIM)
    for l in range(DEPTH):
        x = encoder_layer(x, c, l, cos, sin, params)
    return rms_norm(x, final_g)


def setup_inputs(seed: int = 0) -> dict:
    key = jax.random.key(seed)
    ks = jax.random.split(key, 32)
    D = D_MODEL
    nrm = lambda k, shape, s: jax.random.normal(k, shape, jnp.float32) * s
    return {
        'x_prompt': nrm(ks[0], (BATCH, SEQ, D), 1.0),
        'x_sample': nrm(ks[1], (DEC_BATCH, DEC_SEQ, D), 1.0),
        'c_prompt': nrm(ks[2], (BATCH, D), 1.0),
        'c_sample': nrm(ks[3], (DEC_BATCH, D), 1.0),
        'w_ada': nrm(ks[4], (DEPTH, D, 6 * D), 0.5 * D ** -0.5),
        'b_ada': nrm(ks[5], (DEPTH, 6 * D), 0.02),
        'norm1_g': 1.0 + nrm(ks[6], (DEPTH, D), 0.02),
        'norm2_g': 1.0 + nrm(ks[7], (DEPTH, D), 0.02),
        'w_in': nrm(ks[8], (DEPTH, D, IN_COLS), D ** -0.5),
        'conv_a': nrm(ks[9], (DEPTH, A_CONV, A_WIDTH), A_CONV ** -0.5),
        'w_a_out': nrm(ks[10], (DEPTH, A_WIDTH, D), A_WIDTH ** -0.5),
        'conv_b': nrm(ks[11], (DEPTH, B_CONV, B_WIDTH), B_CONV ** -0.5),
        'ln_b_g': 1.0 + nrm(ks[12], (DEPTH, B_WIDTH), 0.02),
        'ln_b_b': nrm(ks[13], (DEPTH, B_WIDTH), 0.02),
        'w_b_out': nrm(ks[14], (DEPTH, B_WIDTH, D), B_WIDTH ** -0.5),
        'lambda_qk': nrm(ks[15], (DEPTH, 4, C_HEAD_DIM), 0.1),
        'subln_g': 1.0 + nrm(ks[16], (DEPTH, 2 * C_HEAD_DIM), 0.02),
        'w_c_out': nrm(ks[17], (DEPTH, C_WIDTH, D), C_WIDTH ** -0.5),
        'w_out': nrm(ks[18], (DEPTH, D, D), D ** -0.5),
        'peer_wq': nrm(ks[19], (DEPTH, D, PEER_HEADS * PEER_QDIM), D ** -0.5),
        'peer_keys': nrm(ks[20], (DEPTH, PEER_HEADS, 2, PEER_NKEYS, PEER_HALF), PEER_HALF ** -0.5),
        'peer_u': nrm(ks[21], (DEPTH, PEER_EXPERTS, D), D ** -0.5),
        'peer_v': nrm(ks[22], (DEPTH, PEER_EXPERTS, D), 0.5),
        'final_g': 1.0 + nrm(ks[23], (D,), 0.02),
    }


def reference(x_prompt, x_sample, c_prompt, c_sample, w_ada, b_ada, norm1_g, norm2_g, w_in,
              conv_a, w_a_out, conv_b, ln_b_g, ln_b_b, w_b_out, lambda_qk, subln_g, w_c_out,
              w_out, peer_wq, peer_keys, peer_u, peer_v, final_g):
    params = (w_ada, b_ada, norm1_g, norm2_g, w_in, conv_a, w_a_out, conv_b, ln_b_g, ln_b_b,
              w_b_out, lambda_qk, subln_g, w_c_out, w_out, peer_wq, peer_keys, peer_u, peer_v)
    y_prompt = trunk(x_prompt, c_prompt, params, final_g)
    y_sample = trunk(x_sample, c_sample, params, final_g)
    return (y_prompt, y_sample)
```

```python
import functools
import math

import jax
import jax.numpy as jnp
from jax import lax
from jax.experimental import pallas as pl
from jax.experimental.pallas import tpu as pltpu

C_HEAD_DIM = 128
PEER_TOPK = 16
ROPE_THETA = 10000.0
EPS = 1e-6
N_MOD = 6
SUBLANES = 8
LANES = 128
HALO = 2 * SUBLANES
VMEM_LIMIT = 56 * 1024 * 1024

F32 = jnp.float32
BF16 = jnp.bfloat16


def _cparams(*sem):
    return pltpu.CompilerParams(dimension_semantics=sem, vmem_limit_bytes=VMEM_LIMIT)


def _tile(default, *dims):
    t = default
    for d in dims:
        t = math.gcd(t, d)
    return t


def _seq_start(row, starts):
    s = 0
    for b in starts[1:]:
        s = jnp.where(row >= b, b, s)
    return s


def _seq_id(row, starts):
    s = 0
    for b in starts[1:]:
        s = s + (row >= b).astype(jnp.int32)
    return s


def _is_any(row, marks):
    r = row == marks[0]
    for b in marks[1:]:
        r = jnp.logical_or(r, row == b)
    return r


def _ada_kernel(c_ref, w_ref, b_ref, o_ref):
    c = c_ref[...]
    a = (c * jax.nn.sigmoid(c)).astype(BF16)
    w = w_ref[...].astype(BF16)
    o_ref[...] = jnp.dot(a, w, preferred_element_type=F32) + b_ref[...]


def _ada(c_all, w_ada, b_ada):
    depth, d, n = w_ada.shape
    r = c_all.shape[0]
    tn = _tile(512, n)
    return pl.pallas_call(
        _ada_kernel,
        grid=(depth, n // tn),
        in_specs=[
            pl.BlockSpec((r, d), lambda l, j: (0, 0)),
            pl.BlockSpec((None, d, tn), lambda l, j: (l, 0, j)),
            pl.BlockSpec((None, 1, tn), lambda l, j: (l, 0, j)),
        ],
        out_specs=pl.BlockSpec((None, r, tn), lambda l, j: (l, 0, j)),
        out_shape=jax.ShapeDtypeStruct((depth, r, n), F32),
        compiler_params=_cparams("parallel", "parallel"),
        name="ada_mod",
    )(c_all, w_ada, b_ada.reshape(depth, 1, n))


def _norm_kernel(*refs, has_resid, has_mod):
    it = iter(refs)
    x_ref = next(it)
    if has_resid:
        y_ref = next(it)
        gt_ref = next(it)
    g_ref = next(it)
    if has_mod:
        sc_ref = next(it)
        sh_ref = next(it)
    if has_resid:
        xo_ref = next(it)
    h_ref = next(it)

    x = x_ref[...]
    if has_resid:
        x = x + gt_ref[...] * y_ref[...]
        xo_ref[...] = x
    ms = jnp.mean(x * x, axis=-1, keepdims=True)
    h = x * lax.rsqrt(ms + EPS) * g_ref[...]
    if has_mod:
        h = h * (1.0 + sc_ref[...]) + sh_ref[...]
    h_ref[...] = h.astype(h_ref.dtype)


def _norm(x, g, mod, *, starts, mod_layer=None, sc_idx=None, sh_idx=None, resid=None,
          resid_layer=None, gt_idx=None, out_dtype=BF16):
    t, d = x.shape
    tm = _tile(256, *starts[1:], t)
    has_resid = resid is not None
    has_mod = mod_layer is not None

    def mod_spec(layer, chunk):
        def imap(i):
            sid = _seq_id(i * tm, starts)
            return ((layer * SUBLANES + sid) * N_MOD + chunk, 0, 0)
        return pl.BlockSpec((None, 1, d), imap)

    row_spec = pl.BlockSpec((tm, d), lambda i: (i, 0))
    in_specs = [row_spec]
    args = [x]
    if has_resid:
        in_specs += [row_spec, mod_spec(resid_layer, gt_idx)]
        args += [resid, mod]
    in_specs.append(pl.BlockSpec((1, d), lambda i: (0, 0)))
    args.append(g.reshape(1, d))
    if has_mod:
        in_specs += [mod_spec(mod_layer, sc_idx), mod_spec(mod_layer, sh_idx)]
        args += [mod, mod]
    out_specs = []
    out_shape = []
    if has_resid:
        out_specs.append(row_spec)
        out_shape.append(jax.ShapeDtypeStruct((t, d), F32))
    out_specs.append(row_spec)
    out_shape.append(jax.ShapeDtypeStruct((t, d), out_dtype))
    res = pl.pallas_call(
        functools.partial(_norm_kernel, has_resid=has_resid, has_mod=has_mod),
        grid=(t // tm,),
        in_specs=in_specs,
        out_specs=out_specs,
        out_shape=out_shape,
        compiler_params=_cparams("parallel"),
        name="resid_norm",
    )(*args)
    return res if has_resid else (None, res[0])


def _mm_kernel(a_ref, w_ref, o_ref):
    o_ref[...] = jnp.dot(a_ref[...], w_ref[...], preferred_element_type=F32).astype(o_ref.dtype)


def _matmul(a, w, tm=1024, tn=1024):
    m, k = a.shape
    n = w.shape[1]
    tm = _tile(tm, m)
    tn = _tile(tn, n)
    return pl.pallas_call(
        _mm_kernel,
        grid=(m // tm, n // tn),
        in_specs=[pl.BlockSpec((tm, k), lambda i, j: (i, 0)),
                  pl.BlockSpec((k, tn), lambda i, j: (0, j))],
        out_specs=pl.BlockSpec((tm, tn), lambda i, j: (i, j)),
        out_shape=jax.ShapeDtypeStruct((m, n), BF16),
        compiler_params=_cparams("parallel", "parallel"),
        name="in_proj",
    )(a, w)


def _conva_kernel(ah_ref, ab_ref, ac_ref, ahp_ref, acp_ref, ahn_ref, acn_ref, f_ref, o_ref,
                  *, tm, starts, ends):
    i = pl.program_id(0)
    row0 = i * tm
    first = _is_any(row0, starts)
    last = _is_any(row0 + tm, ends)
    w = ac_ref[...].astype(F32) * ah_ref[...].astype(F32)
    wp = (acp_ref[...].astype(F32) * ahp_ref[...].astype(F32))[HALO - 1:HALO, :]
    wn = (acn_ref[...].astype(F32) * ahn_ref[...].astype(F32))[0:1, :]
    wp = jnp.where(first, 0.0, wp)
    wn = jnp.where(last, 0.0, wn)
    rows = lax.broadcasted_iota(jnp.int32, w.shape, 0)
    w_prev = jnp.where(rows == 0, wp, pltpu.roll(w, 1, 0))
    w_next = jnp.where(rows == tm - 1, wn, pltpu.roll(w, tm - 1, 0))
    f = f_ref[...]
    y = w_prev * f[0:1, :] + w * f[1:2, :] + w_next * f[2:3, :]
    o_ref[...] = (ab_ref[...].astype(F32) * y).astype(o_ref.dtype)


def _branch_a(z, conv_a, *, width, starts, ends):
    t = z.shape[0]
    tm = _tile(512, *starts[1:], t)
    tc = _tile(512, width)
    nb = width // tc
    hb = tm // HALO
    last_hb = t // HALO - 1

    def cur(off):
        return pl.BlockSpec((tm, tc), lambda i, j: (i, off * nb + j))

    def prev(off):
        return pl.BlockSpec((HALO, tc), lambda i, j: (jnp.maximum(i * hb - 1, 0), off * nb + j))

    def nxt(off):
        return pl.BlockSpec((HALO, tc),
                            lambda i, j: (jnp.minimum((i + 1) * hb, last_hb), off * nb + j))

    return pl.pallas_call(
        functools.partial(_conva_kernel, tm=tm, starts=starts, ends=ends),
        grid=(t // tm, nb),
        in_specs=[cur(0), cur(1), cur(2), prev(0), prev(2), nxt(0), nxt(2),
                  pl.BlockSpec((conv_a.shape[0], tc), lambda i, j: (0, j))],
        out_specs=pl.BlockSpec((tm, tc), lambda i, j: (i, j)),
        out_shape=jax.ShapeDtypeStruct((t, width), BF16),
        compiler_params=_cparams("parallel", "parallel"),
        name="branch_a",
    )(z, z, z, z, z, z, z, conv_a)


def _convb_kernel(v_ref, g_ref, vp_ref, gp_ref, vn_ref, gn_ref, f_ref, lg_ref, lb_ref, o_ref,
                  ext_ref, acc_ref, *, tm, halo, taps, starts, ends, rc):
    i = pl.program_id(0)
    row0 = i * tm
    first = _is_any(row0, starts)
    last = _is_any(row0 + tm, ends)

    def glu(v, g):
        return v[...].astype(F32) * jax.nn.sigmoid(g[...].astype(F32))

    ext_ref[0:halo, :] = jnp.where(first, 0.0, glu(vp_ref, gp_ref))
    ext_ref[halo:halo + tm, :] = glu(v_ref, g_ref)
    ext_ref[halo + tm:halo + tm + halo, :] = jnp.where(last, 0.0, glu(vn_ref, gn_ref))

    width = v_ref.shape[1]
    half = taps // 2

    def lane_chunk(c, carry):
        col = pl.multiple_of(c * LANES, LANES)
        f = f_ref[:, pl.ds(col, LANES)]
        for r in range(tm // rc):
            acc = jnp.zeros((rc, LANES), F32)
            for k in range(taps):
                start = r * rc + halo - half + k
                acc = acc + ext_ref[pl.ds(start, rc), pl.ds(col, LANES)] * f[k:k + 1, :]
            acc_ref[pl.ds(r * rc, rc), pl.ds(col, LANES)] = acc
        return carry

    lax.fori_loop(0, width // LANES, lane_chunk, 0)

    y = acc_ref[...]
    mu = jnp.mean(y, axis=-1, keepdims=True)
    yc = y - mu
    var = jnp.mean(yc * yc, axis=-1, keepdims=True)
    yn = yc * lax.rsqrt(var + EPS) * lg_ref[...] + lb_ref[...]
    o_ref[...] = (yn * jax.nn.sigmoid(yn)).astype(o_ref.dtype)


def _branch_b(z, conv_b, ln_g, ln_b, *, col_off, width, starts, ends):
    t = z.shape[0]
    taps = conv_b.shape[0]
    halo = HALO
    assert taps // 2 <= halo
    tm = _tile(256, *starts[1:], t)
    rc = _tile(64, tm)
    hb = tm // halo
    last_hb = t // halo - 1
    vb = col_off // width
    gb = vb + 1
    cur = lambda cb: pl.BlockSpec((tm, width), lambda i: (i, cb))
    prev = lambda cb: pl.BlockSpec((halo, width), lambda i: (jnp.maximum(i * hb - 1, 0), cb))
    nxt = lambda cb: pl.BlockSpec((halo, width), lambda i: (jnp.minimum((i + 1) * hb, last_hb), cb))
    full = lambda r: pl.BlockSpec((r, width), lambda i: (0, 0))
    return pl.pallas_call(
        functools.partial(_convb_kernel, tm=tm, halo=halo, taps=taps, starts=starts, ends=ends,
                          rc=rc),
        grid=(t // tm,),
        in_specs=[cur(vb), cur(gb), prev(vb), prev(gb), nxt(vb), nxt(gb),
                  full(taps), full(1), full(1)],
        out_specs=pl.BlockSpec((tm, width), lambda i: (i, 0)),
        out_shape=jax.ShapeDtypeStruct((t, width), BF16),
        scratch_shapes=[pltpu.VMEM((tm + 2 * halo, width), F32), pltpu.VMEM((tm, width), F32)],
        compiler_params=_cparams("parallel"),
        name="branch_b",
    )(z, z, z, z, z, z, conv_b, ln_g.reshape(1, width), ln_b.reshape(1, width))


def _rope_kernel(x_ref, cos_ref, sin_ref, o_ref, *, n_qblocks, scale):
    j = pl.program_id(1)
    s = jnp.where(j < n_qblocks, scale, 1.0).astype(F32)
    cos = cos_ref[...] * s
    sin = sin_ref[...] * s
    for g in range(x_ref.shape[1] // LANES):
        x = x_ref[:, g * LANES:(g + 1) * LANES].astype(F32)
        r = pltpu.roll(x, LANES // 2, 1)
        o_ref[:, g * LANES:(g + 1) * LANES] = (x * cos + r * sin).astype(o_ref.dtype)


def _rope(z, cos_t, sin_t, *, col_off, c_width, starts):
    t = z.shape[0]
    tm = _tile(512, *starts[1:], t)
    tc = _tile(1024, c_width, col_off)
    nq = c_width // tc
    ob = col_off // tc
    scale = C_HEAD_DIM ** -0.5

    def pos(i, j):
        return ((i * tm - _seq_start(i * tm, starts)) // tm, 0)

    return pl.pallas_call(
        functools.partial(_rope_kernel, n_qblocks=nq, scale=scale),
        grid=(t // tm, 2 * nq),
        in_specs=[pl.BlockSpec((tm, tc), lambda i, j: (i, ob + j)),
                  pl.BlockSpec((tm, LANES), pos),
                  pl.BlockSpec((tm, LANES), pos)],
        out_specs=pl.BlockSpec((tm, tc), lambda i, j: (i, j)),
        out_shape=jax.ShapeDtypeStruct((t, 2 * c_width), BF16),
        compiler_params=_cparams("parallel", "parallel"),
        name="rope_qk",
    )(z, cos_t, sin_t)


def _attn_kernel(q_ref, k_ref, v_ref, lq_ref, g_ref, o_ref, m_ref, l_ref, acc_ref,
                 *, lam_init, d):
    ki = pl.program_id(3)

    @pl.when(ki == 0)
    def _():
        m_ref[...] = jnp.full(m_ref.shape, -jnp.inf, F32)
        l_ref[...] = jnp.zeros(l_ref.shape, F32)
        acc_ref[...] = jnp.zeros(acc_ref.shape, F32)

    v = v_ref[...]
    for mp in range(2):
        q = q_ref[:, mp * d:(mp + 1) * d]
        k = k_ref[:, mp * d:(mp + 1) * d]
        s = lax.dot_general(q, k, (((1,), (1,)), ((), ())), preferred_element_type=F32)
        m_prev = m_ref[mp]
        m_new = jnp.maximum(m_prev, jnp.max(s, axis=-1, keepdims=True))
        alpha = jnp.exp(m_prev - m_new)
        p = jnp.exp(s - m_new)
        l_ref[mp] = alpha * l_ref[mp] + jnp.sum(p, axis=-1, keepdims=True)
        acc_ref[mp] = alpha * acc_ref[mp] + jnp.dot(p.astype(v.dtype), v,
                                                    preferred_element_type=F32)
        m_ref[mp] = m_new

    @pl.when(ki == pl.num_programs(3) - 1)
    def _():
        lq = lq_ref[...]
        lam = (jnp.exp(jnp.sum(lq[0:1] * lq[1:2], keepdims=True))
               - jnp.exp(jnp.sum(lq[2:3] * lq[3:4], keepdims=True)) + lam_init)
        o = acc_ref[0] / l_ref[0] - lam * (acc_ref[1] / l_ref[1])
        ms = jnp.mean(o * o, axis=-1, keepdims=True)
        o = o * lax.rsqrt(ms + EPS) * g_ref[...] * (1.0 - lam_init)
        o_ref[...] = o.astype(o_ref.dtype)


def _attention(qk, z, lq, g_sub, *, v_off, heads, row0, batch, seq, lam_init):
    d = C_HEAD_DIM
    hw = 2 * d
    tq = _tile(1024, seq, row0)
    tk = _tile(1024, seq, row0)
    vb = v_off // hw
    nq, nk = seq // tq, seq // tk
    qrow = lambda b, qi: (row0 + b * seq) // tq + qi
    krow = lambda b, ki: (row0 + b * seq) // tk + ki
    orow = lambda b, qi: (b * seq) // tq + qi
    return pl.pallas_call(
        functools.partial(_attn_kernel, lam_init=lam_init, d=d),
        grid=(batch, heads, nq, nk),
        in_specs=[
            pl.BlockSpec((tq, hw), lambda b, h, qi, ki: (qrow(b, qi), h)),
            pl.BlockSpec((tk, hw), lambda b, h, qi, ki: (krow(b, ki), heads + h)),
            pl.BlockSpec((tk, hw), lambda b, h, qi, ki: (krow(b, ki), vb + h)),
            pl.BlockSpec(lq.shape, lambda b, h, qi, ki: (0, 0)),
            pl.BlockSpec((1, hw), lambda b, h, qi, ki: (0, 0)),
        ],
        out_specs=pl.BlockSpec((tq, hw), lambda b, h, qi, ki: (orow(b, qi), h)),
        out_shape=jax.ShapeDtypeStruct((batch * seq, heads * hw), BF16),
        scratch_shapes=[pltpu.VMEM((2, tq, 1), F32), pltpu.VMEM((2, tq, 1), F32),
                        pltpu.VMEM((2, tq, hw), F32)],
        compiler_params=_cparams("parallel", "parallel", "parallel", "arbitrary"),
        name="diff_attn",
    )(qk, qk, z, lq, g_sub.reshape(1, hw))


def _merge_kernel(ua_ref, ub_ref, uc_ref, wa_ref, wb_ref, wc_ref, ga_ref, gb_ref, gc_ref, o_ref):
    def br(u, w, g):
        y = jnp.dot(u[...], w[...], preferred_element_type=F32)
        return jax.nn.sigmoid(g[...].astype(F32)) * y
    o_ref[...] = (br(ua_ref, wa_ref, ga_ref) + br(ub_ref, wb_ref, gb_ref)
                  + br(uc_ref, wc_ref, gc_ref)).astype(o_ref.dtype)


def _merge(ua, ub, uc, wa, wb, wc, z, *, g_off):
    t = ua.shape[0]
    d = wa.shape[1]
    tm = _tile(512, t)
    tn = _tile(512, d, g_off)
    nb = d // tn
    gb = g_off // tn
    u_spec = lambda u: pl.BlockSpec((tm, u.shape[1]), lambda i, j: (i, 0))
    w_spec = lambda w: pl.BlockSpec((w.shape[0], tn), lambda i, j: (0, j))
    g_spec = lambda br: pl.BlockSpec((tm, tn), lambda i, j: (i, gb + br * nb + j))
    return pl.pallas_call(
        _merge_kernel,
        grid=(t // tm, nb),
        in_specs=[u_spec(ua), u_spec(ub), u_spec(uc), w_spec(wa), w_spec(wb), w_spec(wc),
                  g_spec(0), g_spec(1), g_spec(2)],
        out_specs=pl.BlockSpec((tm, tn), lambda i, j: (i, j)),
        out_shape=jax.ShapeDtypeStruct((t, d), BF16),
        compiler_params=_cparams("parallel", "parallel"),
        name="gated_merge",
    )(ua, ub, uc, wa, wb, wc, z, z, z)


def _proj_kernel(a_ref, w_ref, x_ref, gt_ref, o_ref):
    y = jnp.dot(a_ref[...], w_ref[...], preferred_element_type=F32)
    o_ref[...] = x_ref[...] + gt_ref[...] * y


def _out_proj(a, w, x, mod, *, layer, gt_idx, starts):
    t, k = a.shape
    n = w.shape[1]
    tm = _tile(1024, *starts[1:], t)
    tn = _tile(512, n)

    def gt_map(i, j):
        sid = _seq_id(i * tm, starts)
        return ((layer * SUBLANES + sid) * N_MOD + gt_idx, 0, j)

    return pl.pallas_call(
        _proj_kernel,
        grid=(t // tm, n // tn),
        in_specs=[pl.BlockSpec((tm, k), lambda i, j: (i, 0)),
                  pl.BlockSpec((k, tn), lambda i, j: (0, j)),
                  pl.BlockSpec((tm, tn), lambda i, j: (i, j)),
                  pl.BlockSpec((None, 1, tn), gt_map)],
        out_specs=pl.BlockSpec((tm, tn), lambda i, j: (i, j)),
        out_shape=jax.ShapeDtypeStruct((t, n), F32),
        compiler_params=_cparams("parallel", "parallel"),
        name="out_proj",
    )(a, w, x, mod)


def _peer_score_kernel(h_ref, wq_ref, k_ref, o_ref, *, half):
    q = jnp.dot(h_ref[...], wq_ref[...], preferred_element_type=F32).astype(BF16)
    for p in range(2):
        o_ref[p] = lax.dot_general(k_ref[p], q[:, p * half:(p + 1) * half],
                                   (((1,), (1,)), ((), ())), preferred_element_type=F32)


def _peer_scores(h2, wq, keys):
    t, d = h2.shape
    heads, _, nkeys, half = keys.shape
    tm = _tile(1024, t)
    return pl.pallas_call(
        functools.partial(_peer_score_kernel, half=half),
        grid=(t // tm, heads),
        in_specs=[pl.BlockSpec((tm, d), lambda i, h: (i, 0)),
                  pl.BlockSpec((d, 2 * half), lambda i, h: (0, h)),
                  pl.BlockSpec((None, 2, nkeys, half), lambda i, h: (h, 0, 0, 0))],
        out_specs=pl.BlockSpec((None, 2, nkeys, tm), lambda i, h: (h, 0, 0, i)),
        out_shape=jax.ShapeDtypeStruct((heads, 2, nkeys, t), F32),
        compiler_params=_cparams("parallel", "parallel"),
        name="peer_scores",
    )(h2, wq, keys)


def _peer_select_kernel(s_ref, o_ref, a1_ref, a2_ref, *, topk, chunk):
    neg = -jnp.inf
    tt = s_ref.shape[-1]
    for c in range(tt // chunk):
        cols = slice(c * chunk, (c + 1) * chunk)
        for p, a_ref in ((0, a1_ref), (1, a2_ref)):
            x = s_ref[p, :, cols]
            for r in range(topk):
                m = jnp.max(x, axis=0, keepdims=True)
                a_ref[r:r + 1, :] = m
                x = jnp.where(x == m, neg, x)
        a1 = a1_ref[...]
        a2 = a2_ref[...]
        ridx = lax.broadcasted_iota(jnp.int32, a1.shape, 0)
        cands = []
        n_direct = 0
        while (n_direct + 1) * (n_direct + 1) <= topk:
            n_direct += 1
        for i in range(n_direct):
            lim = topk // (i + 1)
            cands.append(jnp.where(ridx < lim, a1_ref[i:i + 1, :] + a2, neg))
        for j in range(topk // (n_direct + 1)):
            lim = topk // (j + 1)
            ok = jnp.logical_and(ridx >= n_direct, ridx < lim)
            cands.append(jnp.where(ok, a1 + a2_ref[j:j + 1, :], neg))
        top = a1_ref[0:1, :] + a2_ref[0:1, :]
        zsum = jnp.zeros_like(top)
        m = top
        for r in range(topk):
            m = cands[0]
            for cnd in cands[1:]:
                m = jnp.maximum(m, cnd)
            m = jnp.max(m, axis=0, keepdims=True)
            zsum = zsum + jnp.exp(m - top)
            if r + 1 < topk:
                cands = [jnp.where(cnd == m, neg, cnd) for cnd in cands]
        o_ref[0:1, cols] = m
        o_ref[1:2, cols] = top + jnp.log(zsum)
    o_ref[2:, :] = jnp.zeros((o_ref.shape[0] - 2, tt), F32)


def _peer_select(s_t):
    heads, _, nkeys, t = s_t.shape
    tt = _tile(1024, t)
    chunk = _tile(256, tt)
    return pl.pallas_call(
        functools.partial(_peer_select_kernel, topk=PEER_TOPK, chunk=chunk),
        grid=(t // tt, heads),
        in_specs=[pl.BlockSpec((None, 2, nkeys, tt), lambda i, h: (h, 0, 0, i))],
        out_specs=pl.BlockSpec((None, SUBLANES, tt), lambda i, h: (h, 0, i)),
        out_shape=jax.ShapeDtypeStruct((heads, SUBLANES, t), F32),
        scratch_shapes=[pltpu.VMEM((PEER_TOPK, chunk), F32), pltpu.VMEM((PEER_TOPK, chunk), F32)],
        compiler_params=_cparams("parallel", "parallel"),
        name="peer_select",
    )(s_t)


def _peer_dense_kernel(h_ref, u_ref, v_ref, s_ref, sel_ref, y_ref, *, heads, nkeys):
    e = pl.program_id(1)

    @pl.when(e == 0)
    def _():
        y_ref[...] = jnp.zeros(y_ref.shape, F32)

    a_t = lax.dot_general(u_ref[...], h_ref[...], (((1,), (1,)), ((), ())),
                          preferred_element_type=F32)
    gl = 0.5 * a_t * (1.0 + lax.erf(a_t * (2.0 ** -0.5)))
    te = u_ref.shape[0]
    rows_per_tile = te // nkeys
    parts = []
    for ii in range(rows_per_tile):
        i_idx = e * rows_per_tile + ii
        w = jnp.zeros((nkeys, h_ref.shape[0]), F32)
        for h in range(heads):
            s1 = s_ref[h, 0, pl.ds(i_idx, 1), :]
            tau = sel_ref[h, 0:1, :]
            cz = sel_ref[h, 1:2, :]
            sm = s1 + s_ref[h, 1]
            w = w + jnp.exp(jnp.where(sm >= tau, sm - cz, -jnp.inf))
        parts.append(gl[ii * nkeys:(ii + 1) * nkeys, :] * w)
    coef_t = parts[0] if len(parts) == 1 else jnp.concatenate(parts, axis=0)
    coef = coef_t.T.astype(BF16)
    y_ref[...] += jnp.dot(coef, v_ref[...], preferred_element_type=F32)


def _peer_dense(h2, u, v, s_t, sel):
    t, d = h2.shape
    ne = u.shape[0]
    heads, _, nkeys, _ = s_t.shape
    tm = _tile(512, t)
    te = _tile(512, ne)
    return pl.pallas_call(
        functools.partial(_peer_dense_kernel, heads=heads, nkeys=nkeys),
        grid=(t // tm, ne // te),
        in_specs=[pl.BlockSpec((tm, d), lambda i, e: (i, 0)),
                  pl.BlockSpec((te, d), lambda i, e: (e, 0)),
                  pl.BlockSpec((te, d), lambda i, e: (e, 0)),
                  pl.BlockSpec((heads, 2, nkeys, tm), lambda i, e: (0, 0, 0, i)),
                  pl.BlockSpec((heads, SUBLANES, tm), lambda i, e: (0, 0, i))],
        out_specs=pl.BlockSpec((tm, d), lambda i, e: (i, 0)),
        out_shape=jax.ShapeDtypeStruct((t, d), F32),
        compiler_params=_cparams("parallel", "arbitrary"),
        name="peer_dense",
    )(h2, u, v, s_t, sel)


def kernel(x_prompt, x_sample, c_prompt, c_sample, w_ada, b_ada, norm1_g, norm2_g, w_in, conv_a, w_a_out, conv_b, ln_b_g, ln_b_b, w_b_out, lambda_qk, subln_g, w_c_out, w_out, peer_wq, peer_keys, peer_u, peer_v, final_g):
    bp, sp, d = x_prompt.shape
    bs, ss, _ = x_sample.shape
    depth = w_in.shape[0]
    a_width = conv_a.shape[-1]
    b_width = conv_b.shape[-1]
    c_width = w_c_out.shape[1]
    heads = c_width // (2 * C_HEAD_DIM)
    col_b = 3 * a_width
    col_c = col_b + 2 * b_width
    col_g = col_c + 3 * c_width

    seq_lens = (sp,) * bp + (ss,) * bs
    starts = tuple(int(sum(seq_lens[:i])) for i in range(len(seq_lens)))
    ends = tuple(s + n for s, n in zip(starts, seq_lens))
    t = ends[-1]
    n_seq = len(seq_lens)
    assert n_seq <= SUBLANES

    x = jnp.concatenate([x_prompt.reshape(bp * sp, d), x_sample.reshape(bs * ss, d)], axis=0)
    c_all = jnp.concatenate([c_prompt, c_sample, jnp.zeros((SUBLANES - n_seq, d), F32)], axis=0)
    mod = _ada(c_all, w_ada, b_ada).reshape(depth * SUBLANES * N_MOD, 1, d)

    smax = max(sp, ss)
    inv = 1.0 / (ROPE_THETA ** (jnp.arange(0, C_HEAD_DIM, 2, dtype=F32) / C_HEAD_DIM))
    ang = jnp.arange(smax, dtype=F32)[:, None] * inv[None, :]
    cos_t = jnp.concatenate([jnp.cos(ang), jnp.cos(ang)], axis=-1)
    sin_t = jnp.concatenate([-jnp.sin(ang), jnp.sin(ang)], axis=-1)

    _, h = _norm(x, norm1_g[0], mod, starts=starts, mod_layer=0, sc_idx=1, sh_idx=0)
    y = None
    for l in range(depth):
        if l > 0:
            x, h = _norm(x, norm1_g[l], mod, starts=starts, mod_layer=l, sc_idx=1, sh_idx=0,
                         resid=y, resid_layer=l - 1, gt_idx=5)
        z = _matmul(h, w_in[l].astype(BF16))
        ua = _branch_a(z, conv_a[l], width=a_width, starts=starts, ends=ends)
        ub = _branch_b(z, conv_b[l], ln_b_g[l], ln_b_b[l], col_off=col_b, width=b_width,
                       starts=starts, ends=ends)
        qk = _rope(z, cos_t, sin_t, col_off=col_c, c_width=c_width, starts=starts)
        lam_init = 0.8 - 0.6 * math.exp(-0.3 * l)
        att = functools.partial(_attention, qk, z, lambda_qk[l], subln_g[l],
                                v_off=col_c + 2 * c_width, heads=heads, lam_init=lam_init)
        uc = jnp.concatenate([att(row0=0, batch=bp, seq=sp),
                              att(row0=bp * sp, batch=bs, seq=ss)], axis=0)
        merged = _merge(ua, ub, uc, w_a_out[l].astype(BF16), w_b_out[l].astype(BF16),
                        w_c_out[l].astype(BF16), z, g_off=col_g)
        x = _out_proj(merged, w_out[l].astype(BF16), x, mod, layer=l, gt_idx=2, starts=starts)
        _, h2 = _norm(x, norm2_g[l], mod, starts=starts, mod_layer=l, sc_idx=4, sh_idx=3)
        s_t = _peer_scores(h2, peer_wq[l].astype(BF16), peer_keys[l].astype(BF16))
        sel = _peer_select(s_t)
        y = _peer_dense(h2, peer_u[l].astype(BF16), peer_v[l].astype(BF16), s_t, sel)
    _, out = _norm(x, final_g, mod, starts=starts, resid=y, resid_layer=depth - 1, gt_idx=5,
                   out_dtype=F32)
    y_prompt = out[:bp * sp].reshape(bp, sp, d)
    y_sample = out[bp * sp:].reshape(bs, ss, d)
    return (y_prompt, y_sample)
```
